```python
import math
import jax, jax.numpy as jnp
from jax import lax
import numpy as np

D_MODEL = 1024
BATCH = 8
SEQ = 8192
DEPTH = 2
DEC_BATCH = 8
DEC_SEQ = 4096
PAST_LEN = 128

GRID_W = 64
QBLK = 128
HEAD_DIM = 64
A_HEADS = 4
A_V_DIM = 2 * HEAD_DIM
B_Q_HEADS = 8
B_KV_HEADS = 2
B_GROUP = B_Q_HEADS // B_KV_HEADS
A_WIDTH = A_HEADS * A_V_DIM
B_WIDTH = B_Q_HEADS * HEAD_DIM
N_BRANCH = 2
D_FF = 4 * D_MODEL
N_BUCKETS = 32
MAX_DISTANCE = 128
ROPE_THETA = 10000.0
AXIS_DIM = HEAD_DIM // 2
EPS = 1e-6
SUBLN_EPS = 1e-5

QA_COLS = A_HEADS * 2 * HEAD_DIM
KA_COLS = A_HEADS * 2 * HEAD_DIM
VA_COLS = A_HEADS * A_V_DIM
QB_COLS = B_Q_HEADS * HEAD_DIM
KB_COLS = B_KV_HEADS * HEAD_DIM
VB_COLS = B_KV_HEADS * HEAD_DIM
GATE_COLS = N_BRANCH * D_MODEL
IN_COLS = QA_COLS + KA_COLS + VA_COLS + QB_COLS + KB_COLS + VB_COLS + GATE_COLS
SPLIT_1 = QA_COLS
SPLIT_2 = SPLIT_1 + KA_COLS
SPLIT_3 = SPLIT_2 + VA_COLS
SPLIT_4 = SPLIT_3 + QB_COLS
SPLIT_5 = SPLIT_4 + KB_COLS
SPLIT_6 = SPLIT_5 + VB_COLS

kernel_name = "hybrid_diffattn_gqa_axial_encoder"


def rmsnorm(x, g, eps=EPS):
    xf = x.astype(jnp.float32)
    y = xf * lax.rsqrt(jnp.mean(xf * xf, axis=-1, keepdims=True) + eps)
    return (y * g.astype(jnp.float32)).astype(x.dtype)


def t5_bucket(rel):
    nb = N_BUCKETS // 2
    max_exact = nb // 2
    ret = (rel > 0).astype(jnp.int32) * nb
    n = jnp.abs(rel)
    nf = jnp.maximum(n, 1).astype(jnp.float32)
    large = max_exact + (jnp.log(nf / max_exact) / math.log(MAX_DISTANCE / max_exact) * (nb - max_exact)).astype(jnp.int32)
    large = jnp.minimum(large, nb - 1)
    return ret + jnp.where(n < max_exact, n, large)


def axial_rope_tables(n):
    rows = n // GRID_W
    row = jnp.repeat(jnp.arange(rows, dtype=jnp.int32), GRID_W).astype(jnp.float32)
    col = jnp.tile(jnp.arange(GRID_W, dtype=jnp.int32), rows).astype(jnp.float32)
    inv = ROPE_THETA ** (-jnp.arange(0, AXIS_DIM, 2, dtype=jnp.float32) / AXIS_DIM)
    ang_r = row[:, None] * inv[None, :]
    ang_c = col[:, None] * inv[None, :]
    return (jnp.cos(ang_r), jnp.sin(ang_r), jnp.cos(ang_c), jnp.sin(ang_c))


def rope_rotate(x, cos, sin):
    half = x.shape[-1] // 2
    x1, x2 = x[..., :half], x[..., half:]
    return jnp.concatenate([x1 * cos - x2 * sin, x2 * cos + x1 * sin], axis=-1)


def axial_rope(x, tabs):
    cos_r, sin_r, cos_c, sin_c = tabs
    shp = (x.shape[1],) + (1,) * (x.ndim - 3) + (cos_r.shape[-1],)
    c = lambda t: t.reshape(shp).astype(x.dtype)
    xr = rope_rotate(x[..., :AXIS_DIM], c(cos_r), c(sin_r))
    xc = rope_rotate(x[..., AXIS_DIM:], c(cos_c), c(sin_c))
    return jnp.concatenate([xr, xc], axis=-1)


def diff_attention(qa, ka, va, t5_table, lam, subln_g, lam_init):
    B, N = qa.shape[0], qa.shape[1]
    nb = N // QBLK
    scale = HEAD_DIM ** -0.5
    qblocks = jnp.moveaxis(qa.reshape(B, nb, QBLK, A_HEADS, 2, HEAD_DIM), 1, 0)
    kpos = jnp.arange(N, dtype=jnp.int32)

    def one(args):
        q, i = args
        qpos = i * QBLK + jnp.arange(QBLK, dtype=jnp.int32)
        bias = t5_table[t5_bucket(kpos[None, :] - qpos[:, None])]
        bias = jnp.transpose(bias, (2, 0, 1)).astype(jnp.float32)
        s = jnp.einsum('bqhcd,bkhcd->bhcqk', q, ka).astype(jnp.float32) * scale + bias[None, :, None]
        p = jax.nn.softmax(s, axis=-1)
        w = p[:, :, 0] - lam * p[:, :, 1]
        return jnp.einsum('bhqk,bkhe->bqhe', w.astype(va.dtype), va)

    o = lax.map(one, (qblocks, jnp.arange(nb, dtype=jnp.int32)))
    o = jnp.moveaxis(o, 0, 1).reshape(B, N, A_HEADS, A_V_DIM)
    o = rmsnorm(o, subln_g, SUBLN_EPS) * (1.0 - lam_init)
    return o.reshape(B, N, A_WIDTH)


def gqa_attention(qb, kb, vb):
    B, N = qb.shape[0], qb.shape[1]
    nb = N // QBLK
    scale = HEAD_DIM ** -0.5
    qblocks = jnp.moveaxis(qb.reshape(B, nb, QBLK, B_KV_HEADS, B_GROUP, HEAD_DIM), 1, 0)

    def one(q):
        s = jnp.einsum('bqkgd,bnkd->bkgqn', q, kb).astype(jnp.float32) * scale
        p = jax.nn.softmax(s, axis=-1)
        return jnp.einsum('bkgqn,bnkd->bqkgd', p.astype(vb.dtype), vb)

    o = lax.map(one, qblocks)
    return jnp.moveaxis(o, 0, 1).reshape(B, N, B_WIDTH)


def encoder(x, t5_table, norm1, w_in, b_gate, lam_q1, lam_k1, lam_q2, lam_k2, subln_g,
            qk_norm_q, qk_norm_k, w_up_a, w_up_b, w_o, norm2, w_ff1, w_ff2, norm_f):
    B, N, _ = x.shape
    tabs = axial_rope_tables(N)
    for l in range(DEPTH):
        lam_init = 0.8 - 0.6 * math.exp(-0.3 * l)
        h = rmsnorm(x, norm1[l])
        z = h @ w_in[l]
        qa, ka, va, qb, kb, vb, zg = jnp.split(z, [SPLIT_1, SPLIT_2, SPLIT_3, SPLIT_4, SPLIT_5, SPLIT_6], axis=-1)
        lam = (jnp.exp(jnp.sum(lam_q1[l].astype(jnp.float32) * lam_k1[l].astype(jnp.float32)))
               - jnp.exp(jnp.sum(lam_q2[l].astype(jnp.float32) * lam_k2[l].astype(jnp.float32))) + lam_init)
        oa = diff_attention(qa.reshape(B, N, A_HEADS, 2, HEAD_DIM), ka.reshape(B, N, A_HEADS, 2, HEAD_DIM),
                            va.reshape(B, N, A_HEADS, A_V_DIM), t5_table, lam, subln_g[l], lam_init)
        qb = axial_rope(rmsnorm(qb.reshape(B, N, B_KV_HEADS, B_GROUP, HEAD_DIM), qk_norm_q[l]), tabs)
        kb = axial_rope(rmsnorm(kb.reshape(B, N, B_KV_HEADS, HEAD_DIM), qk_norm_k[l]), tabs)
        ob = gqa_attention(qb, kb, vb.reshape(B, N, B_KV_HEADS, HEAD_DIM))
        gates = jax.nn.sigmoid((zg + b_gate[l]).astype(jnp.float32)).astype(x.dtype)
        g_a, g_b = jnp.split(gates, 2, axis=-1)
        u = g_a * (oa @ w_up_a[l]) + g_b * (ob @ w_up_b[l])
        x = x + u @ w_o[l]
        h2 = rmsnorm(x, norm2[l])
        x = x + jnp.square(jax.nn.relu(h2 @ w_ff1[l])) @ w_ff2[l]
    return rmsnorm(x, norm_f)


def setup_inputs(seed: int = 0) -> dict:
    key = jax.random.key(seed)
    ks = jax.random.split(key, 24)
    f32 = jnp.float32
    nrm = lambda k, shape, s: jax.random.normal(k, shape, f32) * s
    gain = lambda k, shape: 1.0 + 0.02 * jax.random.normal(k, shape, f32)
    return {
        "x_prompt": jax.random.normal(ks[0], (BATCH, SEQ, D_MODEL), f32),
        "x_sample": jax.random.normal(ks[1], (DEC_BATCH, DEC_SEQ, D_MODEL), f32),
        "t5_table": nrm(ks[2], (N_BUCKETS, A_HEADS), 0.5),
        "norm1": gain(ks[3], (DEPTH, D_MODEL)),
        "w_in": nrm(ks[4], (DEPTH, D_MODEL, IN_COLS), D_MODEL ** -0.5),
        "b_gate": nrm(ks[5], (DEPTH, GATE_COLS), 0.02),
        "lam_q1": nrm(ks[6], (DEPTH, HEAD_DIM), 0.1),
        "lam_k1": nrm(ks[7], (DEPTH, HEAD_DIM), 0.1),
        "lam_q2": nrm(ks[8], (DEPTH, HEAD_DIM), 0.1),
        "lam_k2": nrm(ks[9], (DEPTH, HEAD_DIM), 0.1),
        "subln_g": gain(ks[10], (DEPTH, A_V_DIM)),
        "qk_norm_q": gain(ks[11], (DEPTH, HEAD_DIM)),
        "qk_norm_k": gain(ks[12], (DEPTH, HEAD_DIM)),
        "w_up_a": nrm(ks[13], (DEPTH, A_WIDTH, D_MODEL), A_WIDTH ** -0.5),
        "w_up_b": nrm(ks[14], (DEPTH, B_WIDTH, D_MODEL), B_WIDTH ** -0.5),
        "w_o": nrm(ks[15], (DEPTH, D_MODEL, D_MODEL), D_MODEL ** -0.5),
        "norm2": gain(ks[16], (DEPTH, D_MODEL)),
        "w_ff1": nrm(ks[17], (DEPTH, D_MODEL, D_FF), D_MODEL ** -0.5),
        "w_ff2": nrm(ks[18], (DEPTH, D_FF, D_MODEL), D_FF ** -0.5),
        "norm_f": gain(ks[19], (D_MODEL,)),
    }


def reference(x_prompt, x_sample, t5_table, norm1, w_in, b_gate, lam_q1, lam_k1, lam_q2, lam_k2, subln_g,
              qk_norm_q, qk_norm_k, w_up_a, w_up_b, w_o, norm2, w_ff1, w_ff2, norm_f):
    y_prompt = encoder(x_prompt, t5_table, norm1, w_in, b_gate, lam_q1, lam_k1, lam_q2, lam_k2, subln_g,
                       qk_norm_q, qk_norm_k, w_up_a, w_up_b, w_o, norm2, w_ff1, w_ff2, norm_f)
    y_sample = encoder(x_sample, t5_table, norm1, w_in, b_gate, lam_q1, lam_k1, lam_q2, lam_k2, subln_g,
                       qk_norm_q, qk_norm_k, w_up_a, w_up_b, w_o, norm2, w_ff1, w_ff2, norm_f)
    return (y_prompt, y_sample)
```

```python
import functools
import math

import jax
import jax.numpy as jnp
from jax import lax
from jax.experimental import pallas as pl
from jax.experimental.pallas import tpu as pltpu

D_MODEL = 1024
DEPTH = 2
GRID_W = 64
HEAD_DIM = 64
A_HEADS = 4
B_Q_HEADS = 8
B_KV_HEADS = 2
D_FF = 4 * D_MODEL
N_BUCKETS = 32
ROPE_THETA = 10000.0
AXIS_DIM = HEAD_DIM // 2
EPS = 1e-6
SUBLN_EPS = 1e-5
SCALE = HEAD_DIM ** -0.5

LANES = 128
N_PAIRS = 4
PAIR_COLS = N_PAIRS * LANES
KVB_COLS = B_KV_HEADS * HEAD_DIM
QKV_COLS = 4 * PAIR_COLS + 2 * KVB_COLS
GATE_COLS = 2 * D_MODEL

T5_STARTS = (1, 2, 3, 4, 5, 6, 7, 8, 12, 16, 23, 32, 46, 64, 91)
T5_SIDE = N_BUCKETS // 2

VMEM_LIMIT = 48 * 1024 * 1024

_f32 = jnp.float32
_bf16 = jnp.bfloat16


def _rms(x, g, eps):
    ms = jnp.mean(x * x, axis=-1, keepdims=True)
    return x * lax.rsqrt(ms + eps) * g


def _headnorm_rope(zc, g, cos, sin, seg):
    sq = zc * zc
    hi = sq.astype(_bf16)
    lo = (sq - hi.astype(_f32)).astype(_bf16)
    ms = (jnp.dot(hi, seg, preferred_element_type=_f32)
          + jnp.dot(lo, seg, preferred_element_type=_f32))
    y = zc * lax.rsqrt(ms + EPS) * g
    lane = lax.broadcasted_iota(jnp.int32, cos.shape, 1)
    first_half = (lane & (AXIS_DIM // 2)) == 0
    outs = []
    for c in range(zc.shape[1] // LANES):
        yc = y[:, c * LANES:(c + 1) * LANES]
        partner = jnp.where(first_half,
                            pltpu.roll(yc, LANES - AXIS_DIM // 2, 1),
                            pltpu.roll(yc, AXIS_DIM // 2, 1))
        outs.append(yc * cos + partner * sin)
    return outs[0] if len(outs) == 1 else jnp.concatenate(outs, axis=1)


def _inproj_kernel(x_ref, g1_ref, w_ref, gq_ref, gk_ref, cos_ref, sin_ref, seg_ref,
                   qa_ref, ka_ref, va_ref, qb_ref, kb_ref, vb_ref):
    h = _rms(x_ref[...], g1_ref[...], EPS).astype(_bf16)
    z = jnp.dot(h, w_ref[...], preferred_element_type=_f32)
    c0, c1, c2, c3 = PAIR_COLS, 2 * PAIR_COLS, 3 * PAIR_COLS, 4 * PAIR_COLS
    qa_ref[...] = (z[:, :c0] * SCALE).astype(_bf16)
    ka_ref[...] = z[:, c0:c1].astype(_bf16)
    va_ref[...] = z[:, c1:c2].astype(_bf16)
    cos = cos_ref[...]
    sin = sin_ref[...]
    seg = seg_ref[...]
    qb = _headnorm_rope(z[:, c2:c3], gq_ref[...], cos, sin, seg)
    qb_ref[...] = (qb * SCALE).astype(_bf16)
    kb = _headnorm_rope(z[:, c3:c3 + KVB_COLS], gk_ref[...], cos, sin, seg[:KVB_COLS, :KVB_COLS])
    kb_ref[...] = kb.astype(_bf16)
    vb_ref[...] = z[:, c3 + KVB_COLS:].astype(_bf16)


def _inproj(x2d, n_seq, g1, w_qkv, gq, gk, cos, sin, seg, tm):
    t = x2d.shape[0]
    nt_seq = n_seq // tm
    row = lambda i: (i, 0)
    fixed = lambda i: (0, 0)
    pos = lambda i: (i % nt_seq, 0)
    out_shapes = [jax.ShapeDtypeStruct((t, PAIR_COLS), _bf16)] * 4 + [jax.ShapeDtypeStruct((t, KVB_COLS), _bf16)] * 2
    outs = pl.pallas_call(
        _inproj_kernel,
        grid=(t // tm,),
        in_specs=[
            pl.BlockSpec((tm, D_MODEL), row),
            pl.BlockSpec((1, D_MODEL), fixed),
            pl.BlockSpec((D_MODEL, QKV_COLS), fixed),
            pl.BlockSpec((1, PAIR_COLS), fixed),
            pl.BlockSpec((1, KVB_COLS), fixed),
            pl.BlockSpec((tm, LANES), pos),
            pl.BlockSpec((tm, LANES), pos),
            pl.BlockSpec((PAIR_COLS, PAIR_COLS), fixed),
        ],
        out_specs=[pl.BlockSpec((tm, PAIR_COLS), row)] * 4 + [pl.BlockSpec((tm, KVB_COLS), row)] * 2,
        out_shape=[out_shapes[0], out_shapes[1], out_shapes[2], out_shapes[3], out_shapes[4], out_shapes[5]],
        compiler_params=pltpu.CompilerParams(dimension_semantics=("arbitrary",), vmem_limit_bytes=VMEM_LIMIT),
        name="inproj",
    )(x2d, g1, w_qkv, gq, gk, cos, sin, seg)
    qa, ka, va, qb, kb, vb = outs
    return qa, ka, va, qb, kb, vb


def _attn_kernel(*refs, diff, lam_init, t):
    if diff:
        (tbl_ref, lq1_ref, lk1_ref, lq2_ref, lk2_ref, sg_ref, q_ref, k_ref, v_ref, o_ref,
         qs_ref, m_ref, l_ref, acc_ref, bias_ref) = refs
    else:
        q_ref, k_ref, v_ref, o_ref, qs_ref, m_ref, l_ref, acc_ref = refs
    hd = pl.program_id(0)
    b = pl.program_id(1)
    qi = pl.program_id(2)
    ki = pl.program_id(3)
    nk = pl.num_programs(3)

    if diff:
        @pl.when((b == 0) & (qi == 0) & (ki == 0))
        def _build_bias():
            rows = 8
            col = lax.broadcasted_iota(jnp.int32, (rows, t), 1)
            sub = lax.broadcasted_iota(jnp.int32, (rows, t), 0)
            for o in range(3):
                def body(r, carry, o=o):
                    rel = col - sub - r * rows + (o - 1) * t
                    n = jnp.abs(rel)
                    vneg = jnp.full((rows, t), tbl_ref[0, hd], _f32)
                    vpos = jnp.full((rows, t), tbl_ref[T5_SIDE, hd], _f32)
                    for kb, start in enumerate(T5_STARTS, start=1):
                        ge = n >= start
                        vneg = jnp.where(ge, tbl_ref[kb, hd], vneg)
                        vpos = jnp.where(ge, tbl_ref[T5_SIDE + kb, hd], vpos)
                    bias_ref[o, pl.ds(pl.multiple_of(r * rows, rows), rows), :] = jnp.where(rel > 0, vpos, vneg)
                    return carry
                lax.fori_loop(0, t // rows, body, 0)

    @pl.when(ki == 0)
    def _init():
        q = q_ref[...]
        lane = lax.broadcasted_iota(jnp.int32, q.shape, 1)
        zero = jnp.zeros_like(q)
        qs_ref[:t, :] = jnp.where(lane < HEAD_DIM, q, zero)
        qs_ref[t:, :] = jnp.where(lane < HEAD_DIM, zero, q)
        m_ref[...] = jnp.full(m_ref.shape, -jnp.inf, _f32)
        l_ref[...] = jnp.zeros(l_ref.shape, _f32)
        acc_ref[...] = jnp.zeros(acc_ref.shape, _f32)

    def step(add_bias):
        s = lax.dot_general(qs_ref[...], k_ref[...], (((1,), (1,)), ((), ())),
                            preferred_element_type=_f32)
        s = add_bias(s)
        m_prev = m_ref[...]
        m_new = jnp.maximum(m_prev, jnp.max(s, axis=1, keepdims=True))
        alpha = jnp.exp(m_prev - m_new)
        p = jnp.exp(s - m_new)
        l_ref[...] = alpha * l_ref[...] + jnp.sum(p, axis=1, keepdims=True)
        acc_ref[...] = alpha * acc_ref[...] + jnp.dot(p.astype(_bf16), v_ref[...],
                                                      preferred_element_type=_f32)
        m_ref[...] = m_new

    if diff:
        d = ki - qi
        near = jnp.abs(d) <= 1

        @pl.when(near)
        def _near():
            step(lambda s: (s.reshape(2, t, t) + bias_ref[d + 1][None]).reshape(2 * t, t))

        @pl.when(jnp.logical_not(near))
        def _far():
            far = jnp.where(d > 0, tbl_ref[N_BUCKETS - 1, hd], tbl_ref[T5_SIDE - 1, hd])
            step(lambda s: s + far)
    else:
        step(lambda s: s)

    @pl.when(ki == nk - 1)
    def _finish():
        o = acc_ref[...] / l_ref[...]
        o1 = o[:t]
        o2 = o[t:]
        if diff:
            lam = (jnp.exp(jnp.sum(lq1_ref[...] * lk1_ref[...], keepdims=True))
                   - jnp.exp(jnp.sum(lq2_ref[...] * lk2_ref[...], keepdims=True)) + lam_init)
            od = o1 - lam * o2
            o_ref[...] = (_rms(od, sg_ref[...], SUBLN_EPS) * (1.0 - lam_init)).astype(o_ref.dtype)
        else:
            lane = lax.broadcasted_iota(jnp.int32, o1.shape, 1)
            o_ref[...] = jnp.where(lane < HEAD_DIM, o1, o2).astype(o_ref.dtype)


def _attention(q, k, v, batch, n_seq, t, diff_params=None, lam_init=0.0):
    diff = diff_params is not None
    nt = n_seq // t
    shared_kv = k.shape[1] == LANES
    q_map = lambda h, b, i, j: (b * nt + i, h)
    kv_map = (lambda h, b, i, j: (b * nt + j, 0)) if shared_kv else (lambda h, b, i, j: (b * nt + j, h))
    fixed = lambda h, b, i, j: (0, 0)
    in_specs = [pl.BlockSpec((t, LANES), q_map), pl.BlockSpec((t, LANES), kv_map), pl.BlockSpec((t, LANES), kv_map)]
    scratch = [pltpu.VMEM((2 * t, LANES), _bf16), pltpu.VMEM((2 * t, 1), _f32),
               pltpu.VMEM((2 * t, 1), _f32), pltpu.VMEM((2 * t, LANES), _f32)]
    args = [q, k, v]
    if diff:
        tbl, lq1, lk1, lq2, lk2, sg = diff_params
        in_specs = ([pl.BlockSpec(memory_space=pltpu.SMEM)]
                    + [pl.BlockSpec((1, HEAD_DIM), fixed)] * 4
                    + [pl.BlockSpec((1, LANES), fixed)] + in_specs)
        scratch = scratch + [pltpu.VMEM((3, t, t), _f32)]
        args = [tbl, lq1, lk1, lq2, lk2, sg] + args
    return pl.pallas_call(
        functools.partial(_attn_kernel, diff=diff, lam_init=lam_init, t=t),
        grid=(N_PAIRS, batch, nt, nt),
        in_specs=in_specs,
        out_specs=pl.BlockSpec((t, LANES), q_map),
        out_shape=jax.ShapeDtypeStruct((batch * n_seq, PAIR_COLS), _bf16),
        scratch_shapes=scratch,
        compiler_params=pltpu.CompilerParams(
            dimension_semantics=("arbitrary",) * 4, vmem_limit_bytes=VMEM_LIMIT),
        name="attn_diff" if diff else "attn_gqa",
    )(*args)


def _merge_kernel(x_ref, oa_ref, ob_ref, g1_ref, wg_ref, bg_ref, wua_ref, wub_ref, wo_ref, y_ref):
    x = x_ref[...]
    h = _rms(x, g1_ref[...], EPS).astype(_bf16)
    zg = jnp.dot(h, wg_ref[...], preferred_element_type=_f32) + bg_ref[...]
    gates = 1.0 / (1.0 + jnp.exp(-zg))
    ua = jnp.dot(oa_ref[...], wua_ref[...], preferred_element_type=_f32)
    ub = jnp.dot(ob_ref[...], wub_ref[...], preferred_element_type=_f32)
    u = gates[:, :D_MODEL] * ua + gates[:, D_MODEL:] * ub
    y_ref[...] = x + jnp.dot(u.astype(_bf16), wo_ref[...], preferred_element_type=_f32)


def _merge(x2d, oa, ob, g1, wg, bg, wua, wub, wo, tm):
    t = x2d.shape[0]
    row = lambda i: (i, 0)
    fixed = lambda i: (0, 0)
    return pl.pallas_call(
        _merge_kernel,
        grid=(t // tm,),
        in_specs=[
            pl.BlockSpec((tm, D_MODEL), row),
            pl.BlockSpec((tm, PAIR_COLS), row),
            pl.BlockSpec((tm, PAIR_COLS), row),
            pl.BlockSpec((1, D_MODEL), fixed),
            pl.BlockSpec((D_MODEL, GATE_COLS), fixed),
            pl.BlockSpec((1, GATE_COLS), fixed),
            pl.BlockSpec((PAIR_COLS, D_MODEL), fixed),
            pl.BlockSpec((PAIR_COLS, D_MODEL), fixed),
            pl.BlockSpec((D_MODEL, D_MODEL), fixed),
        ],
        out_specs=pl.BlockSpec((tm, D_MODEL), row),
        out_shape=jax.ShapeDtypeStruct((t, D_MODEL), _f32),
        compiler_params=pltpu.CompilerParams(dimension_semantics=("arbitrary",), vmem_limit_bytes=VMEM_LIMIT),
        name="merge",
    )(x2d, oa, ob, g1, wg, bg, wua, wub, wo)


def _mlp_kernel(x_ref, g2_ref, w1_ref, w2_ref, gf_ref, y_ref, h_ref, acc_ref, *, final_norm):
    j = pl.program_id(1)

    @pl.when(j == 0)
    def _init():
        h_ref[...] = _rms(x_ref[...], g2_ref[...], EPS).astype(_bf16)
        acc_ref[...] = jnp.zeros(acc_ref.shape, _f32)

    a = jnp.dot(h_ref[...], w1_ref[...], preferred_element_type=_f32)
    r = jnp.maximum(a, 0.0)
    acc_ref[...] += jnp.dot((r * r).astype(_bf16), w2_ref[...], preferred_element_type=_f32)

    @pl.when(j == pl.num_programs(1) - 1)
    def _finish():
        y = x_ref[...] + acc_ref[...]
        if final_norm:
            y = _rms(y, gf_ref[...], EPS)
        y_ref[...] = y


def _mlp(x2d, g2, w1, w2, gf, tm, tf, final_norm):
    t = x2d.shape[0]
    return pl.pallas_call(
        functools.partial(_mlp_kernel, final_norm=final_norm),
        grid=(t // tm, D_FF // tf),
        in_specs=[
            pl.BlockSpec((tm, D_MODEL), lambda i, j: (i, 0)),
            pl.BlockSpec((1, D_MODEL), lambda i, j: (0, 0)),
            pl.BlockSpec((D_MODEL, tf), lambda i, j: (0, j)),
            pl.BlockSpec((tf, D_MODEL), lambda i, j: (j, 0)),
            pl.BlockSpec((1, D_MODEL), lambda i, j: (0, 0)),
        ],
        out_specs=pl.BlockSpec((tm, D_MODEL), lambda i, j: (i, 0)),
        out_shape=jax.ShapeDtypeStruct((t, D_MODEL), _f32),
        scratch_shapes=[pltpu.VMEM((tm, D_MODEL), _bf16), pltpu.VMEM((tm, D_MODEL), _f32)],
        compiler_params=pltpu.CompilerParams(
            dimension_semantics=("arbitrary", "arbitrary"), vmem_limit_bytes=VMEM_LIMIT),
        name="mlp",
    )(x2d, g2, w1, w2, gf)


def _rope_tables(n_seq):
    pos = jnp.arange(n_seq, dtype=jnp.int32)
    row = (pos // GRID_W).astype(_f32)
    col = (pos % GRID_W).astype(_f32)
    inv = ROPE_THETA ** (-jnp.arange(0, AXIS_DIM, 2, dtype=_f32) / AXIS_DIM)
    ang_r = row[:, None] * inv[None, :]
    ang_c = col[:, None] * inv[None, :]
    cos = jnp.concatenate([jnp.cos(ang_r)] * 2 + [jnp.cos(ang_c)] * 2, axis=1)
    sin = jnp.concatenate([-jnp.sin(ang_r), jnp.sin(ang_r), -jnp.sin(ang_c), jnp.sin(ang_c)], axis=1)
    return jnp.tile(cos, (1, 2)), jnp.tile(sin, (1, 2))


def _pair_order():
    ar = jnp.arange(HEAD_DIM)
    group = B_Q_HEADS // B_KV_HEADS
    return jnp.concatenate([jnp.concatenate([j * HEAD_DIM + ar, (j + group) * HEAD_DIM + ar]) for j in range(group)])


def _tiles(n_seq):
    t_attn = math.gcd(n_seq, 512)
    tm = math.gcd(n_seq, 512)
    return t_attn, tm


def _encoder(x, p, t_attn=None, tm=None):
    batch, n_seq, _ = x.shape
    d_attn, d_tm = _tiles(n_seq)
    t_attn = t_attn or d_attn
    tm = tm or d_tm
    cos, sin = _rope_tables(n_seq)
    x2d = x.reshape(batch * n_seq, D_MODEL)
    for l in range(DEPTH):
        lam_init = 0.8 - 0.6 * math.exp(-0.3 * l)
        qa, ka, va, qb, kb, vb = _inproj(x2d, n_seq, p["g1"][l], p["w_qkv"][l], p["gq"][l], p["gk"][l],
                                         cos, sin, p["seg"], tm)
        oa = _attention(qa, ka, va, batch, n_seq, t_attn,
                        diff_params=(p["t5"], p["lq1"][l], p["lk1"][l], p["lq2"][l], p["lk2"][l], p["sg"][l]),
                        lam_init=lam_init)
        ob = _attention(qb, kb, vb, batch, n_seq, t_attn)
        x2d = _merge(x2d, oa, ob, p["g1"][l], p["wg"][l], p["bg"][l], p["wua"][l], p["wub"][l], p["wo"][l], tm)
        x2d = _mlp(x2d, p["g2"][l], p["w1"][l], p["w2"][l], p["gf"], tm, min(D_FF, 1024), l == DEPTH - 1)
    return x2d.reshape(batch, n_seq, D_MODEL)


def _prepare(t5_table, norm1, w_in, b_gate, lam_q1, lam_k1, lam_q2, lam_k2, subln_g,
             qk_norm_q, qk_norm_k, w_up_a, w_up_b, w_o, norm2, w_ff1, w_ff2, norm_f):
    perm = _pair_order()
    qb0 = 3 * PAIR_COLS
    w_qkv = jnp.concatenate([w_in[:, :, :qb0], w_in[:, :, qb0:qb0 + PAIR_COLS][:, :, perm],
                             w_in[:, :, qb0 + PAIR_COLS:QKV_COLS]], axis=2).astype(_bf16)
    head = jnp.arange(PAIR_COLS) // HEAD_DIM
    seg = jnp.where(head[:, None] == head[None, :], 1.0 / HEAD_DIM, 0.0).astype(_bf16)
    row = lambda a: a.astype(_f32)[:, None, :]
    return dict(
        t5=t5_table.astype(_f32),
        g1=row(norm1), g2=row(norm2), gf=norm_f.astype(_f32)[None, :],
        w_qkv=w_qkv, wg=w_in[:, :, QKV_COLS:].astype(_bf16), bg=row(b_gate),
        gq=row(jnp.tile(qk_norm_q, (1, B_Q_HEADS))), gk=row(jnp.tile(qk_norm_k, (1, B_KV_HEADS))),
        lq1=row(lam_q1), lk1=row(lam_k1), lq2=row(lam_q2), lk2=row(lam_k2), sg=row(subln_g),
        wua=w_up_a.astype(_bf16), wub=w_up_b[:, perm, :].astype(_bf16), wo=w_o.astype(_bf16),
        w1=w_ff1.astype(_bf16), w2=w_ff2.astype(_bf16), seg=seg,
    )


def kernel(x_prompt, x_sample, t5_table, norm1, w_in, b_gate, lam_q1, lam_k1, lam_q2, lam_k2, subln_g,
           qk_norm_q, qk_norm_k, w_up_a, w_up_b, w_o, norm2, w_ff1, w_ff2, norm_f):
    p = _prepare(t5_table, norm1, w_in, b_gate, lam_q1, lam_k1, lam_q2, lam_k2, subln_g,
                 qk_norm_q, qk_norm_k, w_up_a, w_up_b, w_o, norm2, w_ff1, w_ff2, norm_f)
    return (_encoder(x_prompt, p), _encoder(x_sample, p))
```

```python
import functools
import math

import jax
import jax.numpy as jnp
from jax import lax
from jax.experimental import pallas as pl
from jax.experimental.pallas import tpu as pltpu

D_MODEL = 1024
DEPTH = 2
GRID_W = 64
HEAD_DIM = 64
A_HEADS = 4
B_Q_HEADS = 8
B_KV_HEADS = 2
D_FF = 4 * D_MODEL
N_BUCKETS = 32
ROPE_THETA = 10000.0
AXIS_DIM = HEAD_DIM // 2
EPS = 1e-6
SUBLN_EPS = 1e-5
LOG2E = math.log2(math.e)
Q_SCALE = HEAD_DIM ** -0.5 * LOG2E

LANES = 128
COL_GROUP = 256
N_PAIRS = 4
PAIR_COLS = N_PAIRS * LANES
KVB_COLS = B_KV_HEADS * HEAD_DIM
QKV_COLS = 4 * PAIR_COLS + 2 * KVB_COLS
GATE_COLS = 2 * D_MODEL

T5_STARTS = (1, 2, 3, 4, 5, 6, 7, 8, 12, 16, 23, 32, 46, 64, 91)
T5_SIDE = N_BUCKETS // 2

VMEM_LIMIT = 48 * 1024 * 1024

_f32 = jnp.float32
_bf16 = jnp.bfloat16


def _rms(x, g, eps):
    ms = jnp.mean(x * x, axis=-1, keepdims=True)
    return x * lax.rsqrt(ms + eps) * g


def _headnorm_rope(zc, g, cos, sin, seg):
    sq = zc * zc
    hi = sq.astype(_bf16)
    lo = (sq - hi.astype(_f32)).astype(_bf16)
    ms = (jnp.dot(hi, seg, preferred_element_type=_f32)
          + jnp.dot(lo, seg, preferred_element_type=_f32))
    y = zc * lax.rsqrt(ms + EPS) * g
    lane = lax.broadcasted_iota(jnp.int32, cos.shape, 1)
    first_half = (lane & (AXIS_DIM // 2)) == 0
    outs = []
    for c in range(zc.shape[1] // LANES):
        yc = y[:, c * LANES:(c + 1) * LANES]
        partner = jnp.where(first_half,
                            pltpu.roll(yc, LANES - AXIS_DIM // 2, 1),
                            pltpu.roll(yc, AXIS_DIM // 2, 1))
        outs.append(yc * cos + partner * sin)
    return outs[0] if len(outs) == 1 else jnp.concatenate(outs, axis=1)


def _inproj_kernel(x_ref, g1_ref, w_ref, gq_ref, gk_ref, cos_ref, sin_ref, seg_ref,
                   qa_ref, ka_ref, va_ref, qb_ref, kb_ref, vb_ref):
    h = _rms(x_ref[...], g1_ref[...], EPS).astype(_bf16)
    z = jnp.dot(h, w_ref[...], preferred_element_type=_f32)
    c0, c1, c2, c3 = PAIR_COLS, 2 * PAIR_COLS, 3 * PAIR_COLS, 4 * PAIR_COLS
    qa_ref[0] = (z[:, :c0] * Q_SCALE).T.astype(_bf16)
    ka_ref[...] = z[:, c0:c1].astype(_bf16)
    va_ref[0] = z[:, c1:c2].T.astype(_bf16)
    cos = cos_ref[...]
    sin = sin_ref[...]
    seg = seg_ref[...]
    qb = _headnorm_rope(z[:, c2:c3], gq_ref[...], cos, sin, seg)
    qb_ref[0] = (qb * Q_SCALE).T.astype(_bf16)
    kb = _headnorm_rope(z[:, c3:c3 + KVB_COLS], gk_ref[...], cos, sin, seg[:KVB_COLS, :KVB_COLS])
    kb_ref[...] = kb.astype(_bf16)
    vb_ref[0] = z[:, c3 + KVB_COLS:].T.astype(_bf16)


def _inproj(x2d, n_seq, g1, w_qkv, gq, gk, cos, sin, seg, tm):
    t = x2d.shape[0]
    nt_seq = n_seq // tm
    row = lambda i: (i, 0)
    fixed = lambda i: (0, 0)
    pos = lambda i: (i % nt_seq, 0)
    tile_t = lambda i: (i, 0, 0)
    wide_t = jax.ShapeDtypeStruct((t // tm, PAIR_COLS, tm), _bf16)
    return pl.pallas_call(
        _inproj_kernel,
        grid=(t // tm,),
        in_specs=[
            pl.BlockSpec((tm, D_MODEL), row),
            pl.BlockSpec((1, D_MODEL), fixed),
            pl.BlockSpec((D_MODEL, QKV_COLS), fixed),
            pl.BlockSpec((1, PAIR_COLS), fixed),
            pl.BlockSpec((1, KVB_COLS), fixed),
            pl.BlockSpec((tm, LANES), pos),
            pl.BlockSpec((tm, LANES), pos),
            pl.BlockSpec((PAIR_COLS, PAIR_COLS), fixed),
        ],
        out_specs=[
            pl.BlockSpec((1, PAIR_COLS, tm), tile_t),
            pl.BlockSpec((tm, PAIR_COLS), row),
            pl.BlockSpec((1, PAIR_COLS, tm), tile_t),
            pl.BlockSpec((1, PAIR_COLS, tm), tile_t),
            pl.BlockSpec((tm, KVB_COLS), row),
            pl.BlockSpec((1, KVB_COLS, tm), tile_t),
        ],
        out_shape=[
            wide_t,
            jax.ShapeDtypeStruct((t, PAIR_COLS), _bf16),
            wide_t,
            wide_t,
            jax.ShapeDtypeStruct((t, KVB_COLS), _bf16),
            jax.ShapeDtypeStruct((t // tm, KVB_COLS, tm), _bf16),
        ],
        compiler_params=pltpu.CompilerParams(dimension_semantics=("arbitrary",), vmem_limit_bytes=VMEM_LIMIT),
        name="inproj",
    )(x2d, g1, w_qkv, gq, gk, cos, sin, seg)


def _attn_kernel(*refs, diff, lam_init, t, nt):
    if diff:
        (tbl_ref, lq1_ref, lk1_ref, lq2_ref, lk2_ref, sg_ref, q_ref, k_ref, v_ref, o_ref,
         qs_ref, m_ref, l_ref, acc_ref, *bufs, bias_ref) = refs
    else:
        q_ref, k_ref, v_ref, o_ref, qs_ref, m_ref, l_ref, acc_ref, *bufs = refs
    s_bufs, mc_bufs, p_bufs, alpha_bufs = bufs[0:2], bufs[2:4], bufs[4:6], bufs[6:8]
    hd = pl.program_id(0)
    b = pl.program_id(1)
    qi = pl.program_id(2)

    if diff:
        @pl.when((b == 0) & (qi == 0))
        def _build_bias():
            rows = 8
            col = lax.broadcasted_iota(jnp.int32, (rows, t), 1)
            sub = lax.broadcasted_iota(jnp.int32, (rows, t), 0)
            for o in range(5):
                def body(r, carry, o=o):
                    rel = sub + r * rows - col + (o - 2) * t
                    n = jnp.abs(rel)
                    vneg = jnp.full((rows, t), tbl_ref[0, hd], _f32)
                    vpos = jnp.full((rows, t), tbl_ref[T5_SIDE, hd], _f32)
                    for kb, start in enumerate(T5_STARTS, start=1):
                        ge = n >= start
                        vneg = jnp.where(ge, tbl_ref[kb, hd], vneg)
                        vpos = jnp.where(ge, tbl_ref[T5_SIDE + kb, hd], vpos)
                    bias_ref[o, pl.ds(pl.multiple_of(r * rows, rows), rows), :] = (
                        jnp.where(rel > 0, vpos, vneg) * LOG2E)
                    return carry
                lax.fori_loop(0, t // rows, body, 0)

    q = q_ref[...]
    row = lax.broadcasted_iota(jnp.int32, q.shape, 0)
    zero = jnp.zeros_like(q)
    qs_ref[:, :t] = jnp.where(row < HEAD_DIM, q, zero)
    qs_ref[:, t:] = jnp.where(row < HEAD_DIM, zero, q)
    m_ref[...] = jnp.full(m_ref.shape, -jnp.inf, _f32)
    l_ref[...] = jnp.zeros(l_ref.shape, _f32)
    acc_ref[...] = jnp.zeros(acc_ref.shape, _f32)
    p_bufs[1][...] = jnp.zeros(p_bufs[1].shape, _bf16)
    alpha_bufs[1][...] = jnp.ones(alpha_bufs[1].shape, _f32)

    gw = min(COL_GROUP, t)
    groups = [slice(c0, c0 + gw) for c0 in range(0, 2 * t, gw)]

    def score_stage(j, buf):
        k_t = k_ref[pl.ds(pl.multiple_of(j * t, t), t), :]
        for cols in groups:
            s = jnp.dot(k_t, qs_ref[:, cols], preferred_element_type=_f32)
            if diff:
                b0 = cols.start % t
                s = s + bias_ref[jnp.clip(j - qi, -2, 2) + 2, :, b0:b0 + gw]
            s_bufs[buf][:, cols] = s
            mc_bufs[buf][:, cols] = jnp.max(s, axis=0, keepdims=True)

    def softmax_stage(buf):
        for cols in groups:
            m_prev = m_ref[:, cols]
            m_new = jnp.maximum(m_prev, mc_bufs[buf][:, cols])
            alpha = jnp.exp2(m_prev - m_new)
            p = jnp.exp2(s_bufs[buf][:, cols] - m_new)
            l_ref[:, cols] = alpha * l_ref[:, cols] + jnp.sum(p, axis=0, keepdims=True)
            p_bufs[buf][:, cols] = p.astype(_bf16)
            alpha_bufs[buf][:, cols] = alpha
            m_ref[:, cols] = m_new

    def value_stage(j, buf):
        v_t = v_ref[j]
        for cols in groups:
            acc_ref[:, cols] = alpha_bufs[buf][:, cols] * acc_ref[:, cols] + jnp.dot(
                v_t, p_bufs[buf][:, cols], preferred_element_type=_f32)

    score_stage(0, 0)

    def trip(jj, carry):
        j = 2 * jj
        score_stage(j + 1, 1)
        softmax_stage(0)
        value_stage(jnp.maximum(j - 1, 0), 1)
        score_stage(jnp.minimum(j + 2, nt - 1), 0)
        softmax_stage(1)
        value_stage(j, 0)
        return carry

    lax.fori_loop(0, nt // 2, trip, 0)
    value_stage(nt - 1, 1)

    o = acc_ref[...] * (1.0 / l_ref[...])
    o1 = o[:, :t]
    o2 = o[:, t:]
    if diff:
        lam = (jnp.exp(jnp.sum(lq1_ref[...] * lk1_ref[...], keepdims=True))
               - jnp.exp(jnp.sum(lq2_ref[...] * lk2_ref[...], keepdims=True)) + lam_init)
        od = o1 - lam * o2
        ms = jnp.mean(od * od, axis=0, keepdims=True)
        y = od * lax.rsqrt(ms + SUBLN_EPS) * sg_ref[...] * (1.0 - lam_init)
    else:
        y = jnp.where(row < HEAD_DIM, o1, o2)
    o_ref[...] = y.T.astype(o_ref.dtype)


def _attention(q, k, v, batch, n_seq, t, diff_params=None, lam_init=0.0):
    diff = diff_params is not None
    nt = n_seq // t
    assert nt % 2 == 0
    shared_kv = k.shape[1] == LANES
    fixed = lambda h, b, i: (0, 0)
    in_specs = [
        pl.BlockSpec((None, LANES, t), lambda h, b, i: (b * nt + i, h, 0)),
        pl.BlockSpec((n_seq, LANES), (lambda h, b, i: (b, 0)) if shared_kv else (lambda h, b, i: (b, h))),
        pl.BlockSpec((nt, LANES, t), (lambda h, b, i: (b, 0, 0)) if shared_kv else (lambda h, b, i: (b, h, 0))),
    ]
    scratch = [pltpu.VMEM((LANES, 2 * t), _bf16), pltpu.VMEM((1, 2 * t), _f32),
               pltpu.VMEM((1, 2 * t), _f32), pltpu.VMEM((LANES, 2 * t), _f32),
               pltpu.VMEM((t, 2 * t), _f32), pltpu.VMEM((t, 2 * t), _f32),
               pltpu.VMEM((1, 2 * t), _f32), pltpu.VMEM((1, 2 * t), _f32),
               pltpu.VMEM((t, 2 * t), _bf16), pltpu.VMEM((t, 2 * t), _bf16),
               pltpu.VMEM((1, 2 * t), _f32), pltpu.VMEM((1, 2 * t), _f32)]
    args = [q, k, v]
    if diff:
        tbl, lq1, lk1, lq2, lk2, sg = diff_params
        in_specs = ([pl.BlockSpec(memory_space=pltpu.SMEM)]
                    + [pl.BlockSpec((1, HEAD_DIM), fixed)] * 4
                    + [pl.BlockSpec((LANES, 1), fixed)] + in_specs)
        scratch = scratch + [pltpu.VMEM((5, t, t), _f32)]
        args = [tbl, lq1, lk1, lq2, lk2, sg] + args
    return pl.pallas_call(
        functools.partial(_attn_kernel, diff=diff, lam_init=lam_init, t=t, nt=nt),
        grid=(N_PAIRS, batch, nt),
        in_specs=in_specs,
        out_specs=pl.BlockSpec((t, LANES), lambda h, b, i: (b * nt + i, h)),
        out_shape=jax.ShapeDtypeStruct((batch * n_seq, PAIR_COLS), _bf16),
        scratch_shapes=scratch,
        compiler_params=pltpu.CompilerParams(
            dimension_semantics=("arbitrary",) * 3, vmem_limit_bytes=VMEM_LIMIT),
        name="attn_diff" if diff else "attn_gqa",
    )(*args)


def _merge_kernel(x_ref, oa_ref, ob_ref, g1_ref, wg_ref, bg_ref, wua_ref, wub_ref, wo_ref, y_ref):
    x = x_ref[...]
    h = _rms(x, g1_ref[...], EPS).astype(_bf16)
    zg = jnp.dot(h, wg_ref[...], preferred_element_type=_f32) + bg_ref[...]
    gates = 1.0 / (1.0 + jnp.exp(-zg))
    ua = jnp.dot(oa_ref[...], wua_ref[...], preferred_element_type=_f32)
    ub = jnp.dot(ob_ref[...], wub_ref[...], preferred_element_type=_f32)
    u = gates[:, :D_MODEL] * ua + gates[:, D_MODEL:] * ub
    y_ref[...] = x + jnp.dot(u.astype(_bf16), wo_ref[...], preferred_element_type=_f32)


def _merge(x2d, oa, ob, g1, wg, bg, wua, wub, wo, tm):
    t = x2d.shape[0]
    row = lambda i: (i, 0)
    fixed = lambda i: (0, 0)
    return pl.pallas_call(
        _merge_kernel,
        grid=(t // tm,),
        in_specs=[
            pl.BlockSpec((tm, D_MODEL), row),
            pl.BlockSpec((tm, PAIR_COLS), row),
            pl.BlockSpec((tm, PAIR_COLS), row),
            pl.BlockSpec((1, D_MODEL), fixed),
            pl.BlockSpec((D_MODEL, GATE_COLS), fixed),
            pl.BlockSpec((1, GATE_COLS), fixed),
            pl.BlockSpec((PAIR_COLS, D_MODEL), fixed),
            pl.BlockSpec((PAIR_COLS, D_MODEL), fixed),
            pl.BlockSpec((D_MODEL, D_MODEL), fixed),
        ],
        out_specs=pl.BlockSpec((tm, D_MODEL), row),
        out_shape=jax.ShapeDtypeStruct((t, D_MODEL), _f32),
        compiler_params=pltpu.CompilerParams(dimension_semantics=("arbitrary",), vmem_limit_bytes=VMEM_LIMIT),
        name="merge",
    )(x2d, oa, ob, g1, wg, bg, wua, wub, wo)


def _mlp_kernel(x_ref, g2_ref, w1_ref, w2_ref, gf_ref, y_ref, h_ref, acc_ref, *, final_norm):
    j = pl.program_id(1)

    @pl.when(j == 0)
    def _init():
        h_ref[...] = _rms(x_ref[...], g2_ref[...], EPS).astype(_bf16)
        acc_ref[...] = jnp.zeros(acc_ref.shape, _f32)

    a = jnp.dot(h_ref[...], w1_ref[...], preferred_element_type=_f32)
    r = jnp.maximum(a, 0.0)
    acc_ref[...] += jnp.dot((r * r).astype(_bf16), w2_ref[...], preferred_element_type=_f32)

    @pl.when(j == pl.num_programs(1) - 1)
    def _finish():
        y = x_ref[...] + acc_ref[...]
        if final_norm:
            y = _rms(y, gf_ref[...], EPS)
        y_ref[...] = y


def _mlp(x2d, g2, w1, w2, gf, tm, tf, final_norm):
    t = x2d.shape[0]
    return pl.pallas_call(
        functools.partial(_mlp_kernel, final_norm=final_norm),
        grid=(t // tm, D_FF // tf),
        in_specs=[
            pl.BlockSpec((tm, D_MODEL), lambda i, j: (i, 0)),
            pl.BlockSpec((1, D_MODEL), lambda i, j: (0, 0)),
            pl.BlockSpec((D_MODEL, tf), lambda i, j: (0, j)),
            pl.BlockSpec((tf, D_MODEL), lambda i, j: (j, 0)),
            pl.BlockSpec((1, D_MODEL), lambda i, j: (0, 0)),
        ],
        out_specs=pl.BlockSpec((tm, D_MODEL), lambda i, j: (i, 0)),
        out_shape=jax.ShapeDtypeStruct((t, D_MODEL), _f32),
        scratch_shapes=[pltpu.VMEM((tm, D_MODEL), _bf16), pltpu.VMEM((tm, D_MODEL), _f32)],
        compiler_params=pltpu.CompilerParams(
            dimension_semantics=("arbitrary", "arbitrary"), vmem_limit_bytes=VMEM_LIMIT),
        name="mlp",
    )(x2d, g2, w1, w2, gf)


def _rope_tables(n_seq):
    pos = jnp.arange(n_seq, dtype=jnp.int32)
    row = (pos // GRID_W).astype(_f32)
    col = (pos % GRID_W).astype(_f32)
    inv = ROPE_THETA ** (-jnp.arange(0, AXIS_DIM, 2, dtype=_f32) / AXIS_DIM)
    ang_r = row[:, None] * inv[None, :]
    ang_c = col[:, None] * inv[None, :]
    cos = jnp.concatenate([jnp.cos(ang_r)] * 2 + [jnp.cos(ang_c)] * 2, axis=1)
    sin = jnp.concatenate([-jnp.sin(ang_r), jnp.sin(ang_r), -jnp.sin(ang_c), jnp.sin(ang_c)], axis=1)
    return jnp.tile(cos, (1, 2)), jnp.tile(sin, (1, 2))


def _pair_order():
    ar = jnp.arange(HEAD_DIM)
    group = B_Q_HEADS // B_KV_HEADS
    return jnp.concatenate([jnp.concatenate([j * HEAD_DIM + ar, (j + group) * HEAD_DIM + ar]) for j in range(group)])


def _tile(n_seq):
    return math.gcd(n_seq, 512)


def _encoder(x, p, t_attn=None, tm=None):
    batch, n_seq, _ = x.shape
    t_attn = t_attn or _tile(n_seq)
    tm = tm or t_attn
    assert tm == t_attn
    cos, sin = _rope_tables(n_seq)
    x2d = x.reshape(batch * n_seq, D_MODEL)
    for l in range(DEPTH):
        lam_init = 0.8 - 0.6 * math.exp(-0.3 * l)
        qa, ka, va, qb, kb, vb = _inproj(x2d, n_seq, p["g1"][l], p["w_qkv"][l], p["gq"][l], p["gk"][l],
                                         cos, sin, p["seg"], tm)
        oa = _attention(qa, ka, va, batch, n_seq, t_attn,
                        diff_params=(p["t5"], p["lq1"][l], p["lk1"][l], p["lq2"][l], p["lk2"][l], p["sg"][l]),
                        lam_init=lam_init)
        ob = _attention(qb, kb, vb, batch, n_seq, t_attn)
        x2d = _merge(x2d, oa, ob, p["g1"][l], p["wg"][l], p["bg"][l], p["wua"][l], p["wub"][l], p["wo"][l], tm)
        x2d = _mlp(x2d, p["g2"][l], p["w1"][l], p["w2"][l], p["gf"], tm, min(D_FF, 1024), l == DEPTH - 1)
    return x2d.reshape(batch, n_seq, D_MODEL)


def _prepare(t5_table, norm1, w_in, b_gate, lam_q1, lam_k1, lam_q2, lam_k2, subln_g,
             qk_norm_q, qk_norm_k, w_up_a, w_up_b, w_o, norm2, w_ff1, w_ff2, norm_f):
    perm = _pair_order()
    qb0 = 3 * PAIR_COLS
    w_qkv = jnp.concatenate([w_in[:, :, :qb0], w_in[:, :, qb0:qb0 + PAIR_COLS][:, :, perm],
                             w_in[:, :, qb0 + PAIR_COLS:QKV_COLS]], axis=2).astype(_bf16)
    head = jnp.arange(PAIR_COLS) // HEAD_DIM
    seg = jnp.where(head[:, None] == head[None, :], 1.0 / HEAD_DIM, 0.0).astype(_bf16)
    row = lambda a: a.astype(_f32)[:, None, :]
    return dict(
        t5=t5_table.astype(_f32),
        g1=row(norm1), g2=row(norm2), gf=norm_f.astype(_f32)[None, :],
        w_qkv=w_qkv, wg=w_in[:, :, QKV_COLS:].astype(_bf16), bg=row(b_gate),
        gq=row(jnp.tile(qk_norm_q, (1, B_Q_HEADS))), gk=row(jnp.tile(qk_norm_k, (1, B_KV_HEADS))),
        lq1=row(lam_q1), lk1=row(lam_k1), lq2=row(lam_q2), lk2=row(lam_k2), sg=subln_g.astype(_f32)[:, :, None],
        wua=w_up_a.astype(_bf16), wub=w_up_b[:, perm, :].astype(_bf16), wo=w_o.astype(_bf16),
        w1=w_ff1.astype(_bf16), w2=w_ff2.astype(_bf16), seg=seg,
    )


def kernel(x_prompt, x_sample, t5_table, norm1, w_in, b_gate, lam_q1, lam_k1, lam_q2, lam_k2, subln_g,
           qk_norm_q, qk_norm_k, w_up_a, w_up_b, w_o, norm2, w_ff1, w_ff2, norm_f):
    p = _prepare(t5_table, norm1, w_in, b_gate, lam_q1, lam_k1, lam_q2, lam_k2, subln_g,
                 qk_norm_q, qk_norm_k, w_up_a, w_up_b, w_o, norm2, w_ff1, w_ff2, norm_f)
    return (_encoder(x_prompt, p), _encoder(x_sample, p))
```

```python
import functools
import math

import jax
import jax.numpy as jnp
from jax import lax
from jax.experimental import pallas as pl
from jax.experimental.pallas import tpu as pltpu

D_MODEL = 1024
DEPTH = 2
GRID_W = 64
HEAD_DIM = 64
A_HEADS = 4
B_Q_HEADS = 8
B_KV_HEADS = 2
D_FF = 4 * D_MODEL
N_BUCKETS = 32
ROPE_THETA = 10000.0
AXIS_DIM = HEAD_DIM // 2
EPS = 1e-6
SUBLN_EPS = 1e-5
LOG2E = math.log2(math.e)
Q_SCALE = HEAD_DIM ** -0.5 * LOG2E

LANES = 128
COL_GROUP = 256
N_PAIRS = 4
PAIR_COLS = N_PAIRS * LANES
KVB_COLS = B_KV_HEADS * HEAD_DIM
QKV_COLS = 4 * PAIR_COLS + 2 * KVB_COLS
GATE_COLS = 2 * D_MODEL

T5_STARTS = (1, 2, 3, 4, 5, 6, 7, 8, 12, 16, 23, 32, 46, 64, 91)
T5_SIDE = N_BUCKETS // 2

VMEM_LIMIT = 48 * 1024 * 1024

_f32 = jnp.float32
_bf16 = jnp.bfloat16


def _rms(x, g, eps):
    ms = jnp.mean(x * x, axis=-1, keepdims=True)
    return x * lax.rsqrt(ms + eps) * g


def _headnorm_rope(zc, g, cos, sin, seg):
    sq = zc * zc
    hi = sq.astype(_bf16)
    lo = (sq - hi.astype(_f32)).astype(_bf16)
    ms = (jnp.dot(hi, seg, preferred_element_type=_f32)
          + jnp.dot(lo, seg, preferred_element_type=_f32))
    y = zc * lax.rsqrt(ms + EPS) * g
    lane = lax.broadcasted_iota(jnp.int32, cos.shape, 1)
    first_half = (lane & (AXIS_DIM // 2)) == 0
    outs = []
    for c in range(zc.shape[1] // LANES):
        yc = y[:, c * LANES:(c + 1) * LANES]
        partner = jnp.where(first_half,
                            pltpu.roll(yc, LANES - AXIS_DIM // 2, 1),
                            pltpu.roll(yc, AXIS_DIM // 2, 1))
        outs.append(yc * cos + partner * sin)
    return outs[0] if len(outs) == 1 else jnp.concatenate(outs, axis=1)


def _inproj_kernel(x_ref, g1_ref, w_ref, gq_ref, gk_ref, cos_ref, sin_ref, seg_ref,
                   qa_ref, ka_ref, va_ref, qb_ref, kb_ref, vb_ref):
    h = _rms(x_ref[...], g1_ref[...], EPS).astype(_bf16)
    z = jnp.dot(h, w_ref[...], preferred_element_type=_f32)
    c0, c1, c2, c3 = PAIR_COLS, 2 * PAIR_COLS, 3 * PAIR_COLS, 4 * PAIR_COLS
    qa_ref[0] = (z[:, :c0] * Q_SCALE).T.astype(_bf16)
    ka_ref[...] = z[:, c0:c1].astype(_bf16)
    va_ref[0] = z[:, c1:c2].T.astype(_bf16)
    cos = cos_ref[...]
    sin = sin_ref[...]
    seg = seg_ref[...]
    qb = _headnorm_rope(z[:, c2:c3], gq_ref[...], cos, sin, seg)
    qb_ref[0] = (qb * Q_SCALE).T.astype(_bf16)
    kb = _headnorm_rope(z[:, c3:c3 + KVB_COLS], gk_ref[...], cos, sin, seg[:KVB_COLS, :KVB_COLS])
    kb_ref[...] = kb.astype(_bf16)
    vb_ref[0] = z[:, c3 + KVB_COLS:].T.astype(_bf16)


def _inproj(x2d, n_seq, g1, w_qkv, gq, gk, cos, sin, seg, tm):
    t = x2d.shape[0]
    nt_seq = n_seq // tm
    row = lambda i: (i, 0)
    fixed = lambda i: (0, 0)
    pos = lambda i: (i % nt_seq, 0)
    tile_t = lambda i: (i, 0, 0)
    wide_t = jax.ShapeDtypeStruct((t // tm, PAIR_COLS, tm), _bf16)
    return pl.pallas_call(
        _inproj_kernel,
        grid=(t // tm,),
        in_specs=[
            pl.BlockSpec((tm, D_MODEL), row),
            pl.BlockSpec((1, D_MODEL), fixed),
            pl.BlockSpec((D_MODEL, QKV_COLS), fixed),
            pl.BlockSpec((1, PAIR_COLS), fixed),
            pl.BlockSpec((1, KVB_COLS), fixed),
            pl.BlockSpec((tm, LANES), pos),
            pl.BlockSpec((tm, LANES), pos),
            pl.BlockSpec((PAIR_COLS, PAIR_COLS), fixed),
        ],
        out_specs=[
            pl.BlockSpec((1, PAIR_COLS, tm), tile_t),
            pl.BlockSpec((tm, PAIR_COLS), row),
            pl.BlockSpec((1, PAIR_COLS, tm), tile_t),
            pl.BlockSpec((1, PAIR_COLS, tm), tile_t),
            pl.BlockSpec((tm, KVB_COLS), row),
            pl.BlockSpec((1, KVB_COLS, tm), tile_t),
        ],
        out_shape=[
            wide_t,
            jax.ShapeDtypeStruct((t, PAIR_COLS), _bf16),
            wide_t,
            wide_t,
            jax.ShapeDtypeStruct((t, KVB_COLS), _bf16),
            jax.ShapeDtypeStruct((t // tm, KVB_COLS, tm), _bf16),
        ],
        compiler_params=pltpu.CompilerParams(dimension_semantics=("arbitrary",), vmem_limit_bytes=VMEM_LIMIT),
        name="inproj",
    )(x2d, g1, w_qkv, gq, gk, cos, sin, seg)


def _attn_kernel(*refs, diff, lam_init, t, nt):
    if diff:
        (tbl_ref, lq1_ref, lk1_ref, lq2_ref, lk2_ref, sg_ref, q_ref, k_ref, v_ref, o_ref,
         qs_ref, m_ref, l_ref, acc_ref, *bufs, bias_ref, shift_ref) = refs
    else:
        q_ref, k_ref, v_ref, o_ref, qs_ref, m_ref, l_ref, acc_ref, *bufs = refs
    s_bufs, mc_bufs, p_bufs, alpha_bufs = bufs[0:2], bufs[2:4], bufs[4:6], bufs[6:8]
    hd = pl.program_id(0)
    b = pl.program_id(1)
    qi = pl.program_id(2)

    if diff:
        @pl.when((b == 0) & (qi == 0))
        def _build_bias():
            rows = 8
            col = lax.broadcasted_iota(jnp.int32, (rows, t), 1)
            sub = lax.broadcasted_iota(jnp.int32, (rows, t), 0)
            for o in range(5):
                def body(r, carry, o=o):
                    rel = sub + r * rows - col + (o - 2) * t
                    n = jnp.abs(rel)
                    vneg = jnp.full((rows, t), tbl_ref[0, hd], _f32)
                    vpos = jnp.full((rows, t), tbl_ref[T5_SIDE, hd], _f32)
                    for kb, start in enumerate(T5_STARTS, start=1):
                        ge = n >= start
                        vneg = jnp.where(ge, tbl_ref[kb, hd], vneg)
                        vpos = jnp.where(ge, tbl_ref[T5_SIDE + kb, hd], vpos)
                    bias_ref[o, pl.ds(pl.multiple_of(r * rows, rows), rows), :] = (
                        jnp.where(rel > 0, vpos, vneg) * LOG2E)
                    return carry
                lax.fori_loop(0, t // rows, body, 0)

    if diff:
        far_left = tbl_ref[T5_SIDE - 1, hd] * LOG2E
        far_right = tbl_ref[N_BUCKETS - 1, hd] * LOG2E

    q = q_ref[...]
    row = lax.broadcasted_iota(jnp.int32, q.shape, 0)
    zero = jnp.zeros_like(q)
    qs_ref[:, :t] = jnp.where(row < HEAD_DIM, q, zero)
    qs_ref[:, t:] = jnp.where(row < HEAD_DIM, zero, q)
    m_ref[...] = jnp.full(m_ref.shape, -jnp.inf, _f32)
    l_ref[...] = jnp.zeros(l_ref.shape, _f32)
    acc_ref[...] = jnp.zeros(acc_ref.shape, _f32)
    p_bufs[1][...] = jnp.zeros(p_bufs[1].shape, _bf16)
    alpha_bufs[1][...] = jnp.ones(alpha_bufs[1].shape, _f32)

    gw = min(COL_GROUP, t)
    groups = [slice(c0, c0 + gw) for c0 in range(0, 2 * t, gw)]

    def score_stage(j, buf, far=False):
        k_t = k_ref[pl.ds(pl.multiple_of(j * t, t), t), :]
        shift = None
        if diff:
            shift = jnp.where(j > qi, far_right, far_left) if far else jnp.zeros((), _f32)
            shift_ref[buf] = shift
        for cols in groups:
            s = jnp.dot(k_t, qs_ref[:, cols], preferred_element_type=_f32)
            if diff and not far:
                b0 = cols.start % t
                s = s + bias_ref[jnp.clip(j - qi, -2, 2) + 2, :, b0:b0 + gw]
            s_bufs[buf][:, cols] = s
            mc = jnp.max(s, axis=0, keepdims=True)
            mc_bufs[buf][:, cols] = mc + shift if far else mc

    def softmax_stage(buf):
        for cols in groups:
            m_prev = m_ref[:, cols]
            m_new = jnp.maximum(m_prev, mc_bufs[buf][:, cols])
            alpha = jnp.exp2(m_prev - m_new)
            sub = m_new - shift_ref[buf] if diff else m_new
            p = jnp.exp2(s_bufs[buf][:, cols] - sub)
            l_ref[:, cols] = alpha * l_ref[:, cols] + jnp.sum(p, axis=0, keepdims=True)
            p_bufs[buf][:, cols] = p.astype(_bf16)
            alpha_bufs[buf][:, cols] = alpha
            m_ref[:, cols] = m_new

    def value_stage(j, buf):
        v_t = v_ref[j]
        for cols in groups:
            acc_ref[:, cols] = alpha_bufs[buf][:, cols] * acc_ref[:, cols] + jnp.dot(
                v_t, p_bufs[buf][:, cols], preferred_element_type=_f32)

    def trip(j, far=False, last=False):
        score_stage(j + 1, 1, far)
        softmax_stage(0)
        value_stage(jnp.maximum(j - 1, 0), 1)
        if not last:
            score_stage(j + 2, 0, far)
        softmax_stage(1)
        value_stage(j, 0)

    def loop_trip(jj, carry):
        j = 2 * jj
        if diff:
            all_far = (j + 1 - qi >= 2) | (j + 2 - qi <= -2)
            lax.cond(all_far, lambda: trip(j, far=True), lambda: trip(j))
        else:
            trip(j)
        return carry

    score_stage(0, 0)
    lax.fori_loop(0, nt // 2 - 1, loop_trip, 0)
    trip(nt - 2, last=True)
    value_stage(nt - 1, 1)

    o = acc_ref[...] * (1.0 / l_ref[...])
    o1 = o[:, :t]
    o2 = o[:, t:]
    if diff:
        lam = (jnp.exp(jnp.sum(lq1_ref[...] * lk1_ref[...], keepdims=True))
               - jnp.exp(jnp.sum(lq2_ref[...] * lk2_ref[...], keepdims=True)) + lam_init)
        od = o1 - lam * o2
        ms = jnp.mean(od * od, axis=0, keepdims=True)
        y = od * lax.rsqrt(ms + SUBLN_EPS) * sg_ref[...] * (1.0 - lam_init)
    else:
        y = jnp.where(row < HEAD_DIM, o1, o2)
    o_ref[...] = y.T.astype(o_ref.dtype)


def _attention(q, k, v, batch, n_seq, t, diff_params=None, lam_init=0.0):
    diff = diff_params is not None
    nt = n_seq // t
    assert nt % 2 == 0
    shared_kv = k.shape[1] == LANES
    fixed = lambda h, b, i: (0, 0)
    in_specs = [
        pl.BlockSpec((None, LANES, t), lambda h, b, i: (b * nt + i, h, 0)),
        pl.BlockSpec((n_seq, LANES), (lambda h, b, i: (b, 0)) if shared_kv else (lambda h, b, i: (b, h))),
        pl.BlockSpec((nt, LANES, t), (lambda h, b, i: (b, 0, 0)) if shared_kv else (lambda h, b, i: (b, h, 0))),
    ]
    scratch = [pltpu.VMEM((LANES, 2 * t), _bf16), pltpu.VMEM((1, 2 * t), _f32),
               pltpu.VMEM((1, 2 * t), _f32), pltpu.VMEM((LANES, 2 * t), _f32),
               pltpu.VMEM((t, 2 * t), _f32), pltpu.VMEM((t, 2 * t), _f32),
               pltpu.VMEM((1, 2 * t), _f32), pltpu.VMEM((1, 2 * t), _f32),
               pltpu.VMEM((t, 2 * t), _bf16), pltpu.VMEM((t, 2 * t), _bf16),
               pltpu.VMEM((1, 2 * t), _f32), pltpu.VMEM((1, 2 * t), _f32)]
    args = [q, k, v]
    if diff:
        tbl, lq1, lk1, lq2, lk2, sg = diff_params
        in_specs = ([pl.BlockSpec(memory_space=pltpu.SMEM)]
                    + [pl.BlockSpec((1, HEAD_DIM), fixed)] * 4
                    + [pl.BlockSpec((LANES, 1), fixed)] + in_specs)
        scratch = scratch + [pltpu.VMEM((5, t, t), _f32), pltpu.SMEM((2,), _f32)]
        args = [tbl, lq1, lk1, lq2, lk2, sg] + args
    return pl.pallas_call(
        functools.partial(_attn_kernel, diff=diff, lam_init=lam_init, t=t, nt=nt),
        grid=(N_PAIRS, batch, nt),
        in_specs=in_specs,
        out_specs=pl.BlockSpec((t, LANES), lambda h, b, i: (b * nt + i, h)),
        out_shape=jax.ShapeDtypeStruct((batch * n_seq, PAIR_COLS), _bf16),
        scratch_shapes=scratch,
        compiler_params=pltpu.CompilerParams(
            dimension_semantics=("arbitrary",) * 3, vmem_limit_bytes=VMEM_LIMIT),
        name="attn_diff" if diff else "attn_gqa",
    )(*args)


def _merge_kernel(x_ref, oa_ref, ob_ref, g1_ref, wg_ref, bg_ref, wua_ref, wub_ref, wo_ref, y_ref):
    x = x_ref[...]
    h = _rms(x, g1_ref[...], EPS).astype(_bf16)
    zg = jnp.dot(h, wg_ref[...], preferred_element_type=_f32) + bg_ref[...]
    gates = 1.0 / (1.0 + jnp.exp(-zg))
    ua = jnp.dot(oa_ref[...], wua_ref[...], preferred_element_type=_f32)
    ub = jnp.dot(ob_ref[...], wub_ref[...], preferred_element_type=_f32)
    u = gates[:, :D_MODEL] * ua + gates[:, D_MODEL:] * ub
    y_ref[...] = x + jnp.dot(u.astype(_bf16), wo_ref[...], preferred_element_type=_f32)


def _merge(x2d, oa, ob, g1, wg, bg, wua, wub, wo, tm):
    t = x2d.shape[0]
    row = lambda i: (i, 0)
    fixed = lambda i: (0, 0)
    return pl.pallas_call(
        _merge_kernel,
        grid=(t // tm,),
        in_specs=[
            pl.BlockSpec((tm, D_MODEL), row),
            pl.BlockSpec((tm, PAIR_COLS), row),
            pl.BlockSpec((tm, PAIR_COLS), row),
            pl.BlockSpec((1, D_MODEL), fixed),
            pl.BlockSpec((D_MODEL, GATE_COLS), fixed),
            pl.BlockSpec((1, GATE_COLS), fixed),
            pl.BlockSpec((PAIR_COLS, D_MODEL), fixed),
            pl.BlockSpec((PAIR_COLS, D_MODEL), fixed),
            pl.BlockSpec((D_MODEL, D_MODEL), fixed),
        ],
        out_specs=pl.BlockSpec((tm, D_MODEL), row),
        out_shape=jax.ShapeDtypeStruct((t, D_MODEL), _f32),
        compiler_params=pltpu.CompilerParams(dimension_semantics=("arbitrary",), vmem_limit_bytes=VMEM_LIMIT),
        name="merge",
    )(x2d, oa, ob, g1, wg, bg, wua, wub, wo)


def _mlp_kernel(x_ref, g2_ref, w1_ref, w2_ref, gf_ref, y_ref, h_ref, acc_ref, *, final_norm):
    j = pl.program_id(1)

    @pl.when(j == 0)
    def _init():
        h_ref[...] = _rms(x_ref[...], g2_ref[...], EPS).astype(_bf16)
        acc_ref[...] = jnp.zeros(acc_ref.shape, _f32)

    a = jnp.dot(h_ref[...], w1_ref[...], preferred_element_type=_f32)
    r = jnp.maximum(a, 0.0)
    acc_ref[...] += jnp.dot((r * r).astype(_bf16), w2_ref[...], preferred_element_type=_f32)

    @pl.when(j == pl.num_programs(1) - 1)
    def _finish():
        y = x_ref[...] + acc_ref[...]
        if final_norm:
            y = _rms(y, gf_ref[...], EPS)
        y_ref[...] = y


def _mlp(x2d, g2, w1, w2, gf, tm, tf, final_norm):
    t = x2d.shape[0]
    return pl.pallas_call(
        functools.partial(_mlp_kernel, final_norm=final_norm),
        grid=(t // tm, D_FF // tf),
        in_specs=[
            pl.BlockSpec((tm, D_MODEL), lambda i, j: (i, 0)),
            pl.BlockSpec((1, D_MODEL), lambda i, j: (0, 0)),
            pl.BlockSpec((D_MODEL, tf), lambda i, j: (0, j)),
            pl.BlockSpec((tf, D_MODEL), lambda i, j: (j, 0)),
            pl.BlockSpec((1, D_MODEL), lambda i, j: (0, 0)),
        ],
        out_specs=pl.BlockSpec((tm, D_MODEL), lambda i, j: (i, 0)),
        out_shape=jax.ShapeDtypeStruct((t, D_MODEL), _f32),
        scratch_shapes=[pltpu.VMEM((tm, D_MODEL), _bf16), pltpu.VMEM((tm, D_MODEL), _f32)],
        compiler_params=pltpu.CompilerParams(
            dimension_semantics=("arbitrary", "arbitrary"), vmem_limit_bytes=VMEM_LIMIT),
        name="mlp",
    )(x2d, g2, w1, w2, gf)


def _rope_tables(n_seq):
    pos = jnp.arange(n_seq, dtype=jnp.int32)
    row = (pos // GRID_W).astype(_f32)
    col = (pos % GRID_W).astype(_f32)
    inv = ROPE_THETA ** (-jnp.arange(0, AXIS_DIM, 2, dtype=_f32) / AXIS_DIM)
    ang_r = row[:, None] * inv[None, :]
    ang_c = col[:, None] * inv[None, :]
    cos = jnp.concatenate([jnp.cos(ang_r)] * 2 + [jnp.cos(ang_c)] * 2, axis=1)
    sin = jnp.concatenate([-jnp.sin(ang_r), jnp.sin(ang_r), -jnp.sin(ang_c), jnp.sin(ang_c)], axis=1)
    return jnp.tile(cos, (1, 2)), jnp.tile(sin, (1, 2))


def _pair_order():
    ar = jnp.arange(HEAD_DIM)
    group = B_Q_HEADS // B_KV_HEADS
    return jnp.concatenate([jnp.concatenate([j * HEAD_DIM + ar, (j + group) * HEAD_DIM + ar]) for j in range(group)])


def _tile(n_seq):
    return math.gcd(n_seq, 512)


def _encoder(x, p, t_attn=None, tm=None):
    batch, n_seq, _ = x.shape
    t_attn = t_attn or _tile(n_seq)
    tm = tm or t_attn
    assert tm == t_attn
    cos, sin = _rope_tables(n_seq)
    x2d = x.reshape(batch * n_seq, D_MODEL)
    for l in range(DEPTH):
        lam_init = 0.8 - 0.6 * math.exp(-0.3 * l)
        qa, ka, va, qb, kb, vb = _inproj(x2d, n_seq, p["g1"][l], p["w_qkv"][l], p["gq"][l], p["gk"][l],
                                         cos, sin, p["seg"], tm)
        oa = _attention(qa, ka, va, batch, n_seq, t_attn,
                        diff_params=(p["t5"], p["lq1"][l], p["lk1"][l], p["lq2"][l], p["lk2"][l], p["sg"][l]),
                        lam_init=lam_init)
        ob = _attention(qb, kb, vb, batch, n_seq, t_attn)
        x2d = _merge(x2d, oa, ob, p["g1"][l], p["wg"][l], p["bg"][l], p["wua"][l], p["wub"][l], p["wo"][l], tm)
        x2d = _mlp(x2d, p["g2"][l], p["w1"][l], p["w2"][l], p["gf"], tm, min(D_FF, 1024), l == DEPTH - 1)
    return x2d.reshape(batch, n_seq, D_MODEL)


def _prepare(t5_table, norm1, w_in, b_gate, lam_q1, lam_k1, lam_q2, lam_k2, subln_g,
             qk_norm_q, qk_norm_k, w_up_a, w_up_b, w_o, norm2, w_ff1, w_ff2, norm_f):
    perm = _pair_order()
    qb0 = 3 * PAIR_COLS
    w_qkv = jnp.concatenate([w_in[:, :, :qb0], w_in[:, :, qb0:qb0 + PAIR_COLS][:, :, perm],
                             w_in[:, :, qb0 + PAIR_COLS:QKV_COLS]], axis=2).astype(_bf16)
    head = jnp.arange(PAIR_COLS) // HEAD_DIM
    seg = jnp.where(head[:, None] == head[None, :], 1.0 / HEAD_DIM, 0.0).astype(_bf16)
    row = lambda a: a.astype(_f32)[:, None, :]
    return dict(
        t5=t5_table.astype(_f32),
        g1=row(norm1), g2=row(norm2), gf=norm_f.astype(_f32)[None, :],
        w_qkv=w_qkv, wg=w_in[:, :, QKV_COLS:].astype(_bf16), bg=row(b_gate),
        gq=row(jnp.tile(qk_norm_q, (1, B_Q_HEADS))), gk=row(jnp.tile(qk_norm_k, (1, B_KV_HEADS))),
        lq1=row(lam_q1), lk1=row(lam_k1), lq2=row(lam_q2), lk2=row(lam_k2), sg=subln_g.astype(_f32)[:, :, None],
        wua=w_up_a.astype(_bf16), wub=w_up_b[:, perm, :].astype(_bf16), wo=w_o.astype(_bf16),
        w1=w_ff1.astype(_bf16), w2=w_ff2.astype(_bf16), seg=seg,
    )


def kernel(x_prompt, x_sample, t5_table, norm1, w_in, b_gate, lam_q1, lam_k1, lam_q2, lam_k2, subln_g,
           qk_norm_q, qk_norm_k, w_up_a, w_up_b, w_o, norm2, w_ff1, w_ff2, norm_f):
    p = _prepare(t5_table, norm1, w_in, b_gate, lam_q1, lam_k1, lam_q2, lam_k2, subln_g,
                 qk_norm_q, qk_norm_k, w_up_a, w_up_b, w_o, norm2, w_ff1, w_ff2, norm_f)
    return (_encoder(x_prompt, p), _encoder(x_sample, p))
```
